```python
import jax, jax.numpy as jnp
from jax import lax
import numpy as np

D_MODEL = 1024
BATCH = 32
SEQ = 2048
DEPTH = 1
DEC_BATCH = 8
DEC_SEQ = 4096
PAST_LEN = 128

D_MIX = D_MODEL
D_ATT = D_MIX // 2
N_HEADS_ATT = 8
HD_ATT = D_ATT // N_HEADS_ATT
D_MLSTM = D_MIX - D_ATT
N_HEADS_M = 4
HD_M = D_MLSTM // N_HEADS_M
N_GATES = 4 * N_HEADS_M
SPLIT_SIZES = (D_ATT, D_ATT, D_ATT, 2 * D_MLSTM, D_MLSTM, D_MLSTM, N_GATES)
D_IN = 3 * D_ATT + 4 * D_MLSTM + N_GATES
D_FF = 2816
CONV_W = 3
GRID_W = 64
WIN_ROWS = 8
WIN_COLS = 16
Q_BLOCK_COLS = 16
K_BLOCK_COLS = Q_BLOCK_COLS + WIN_COLS
CHUNK = 64
EPS = 1e-6
NEG_INF = -1e30

kernel_name = 'hymba_natten_mlstm_encoder'


def rms_norm(x, g):
    xf = x.astype(jnp.float32)
    y = xf * lax.rsqrt(jnp.mean(xf * xf, axis=-1, keepdims=True) + EPS)
    return (y * g.astype(jnp.float32)).astype(x.dtype)


def dwconv_centered(x, w):
    T = x.shape[1]
    pad = CONV_W // 2
    xp = jnp.pad(x, ((0, 0), (pad, CONV_W - 1 - pad), (0, 0)))
    y = xp[:, 0:T] * w[0]
    for j in range(1, CONV_W):
        y = y + xp[:, j:j + T] * w[j]
    return y


def neighbourhood_attention(q, k, v, rpb):
    B, T, H, d = q.shape
    rows = T // GRID_W
    kr = min(WIN_ROWS, rows)
    qg = q.reshape(B, rows, GRID_W, H, d)
    kg = k.reshape(B, rows, GRID_W, H, d)
    vg = v.reshape(B, rows, GRID_W, H, d)
    r = np.arange(rows)
    row_start = np.clip(r - WIN_ROWS // 2, 0, rows - kr)
    row_idx = row_start[:, None] + np.arange(kr)[None, :]
    k_rows = kg[:, row_idx]
    v_rows = vg[:, row_idx]
    dr = row_idx - r[:, None] + (WIN_ROWS - 1)
    scale = d ** -0.5
    outs = []
    for c0 in range(0, GRID_W, Q_BLOCK_COLS):
        ks = min(max(c0 - WIN_COLS // 2, 0), GRID_W - K_BLOCK_COLS)
        qc = np.arange(c0, c0 + Q_BLOCK_COLS)
        kc = np.arange(ks, ks + K_BLOCK_COLS)
        cstart = np.clip(qc - WIN_COLS // 2, 0, GRID_W - WIN_COLS)
        valid = (kc[None, :] >= cstart[:, None]) & (kc[None, :] < cstart[:, None] + WIN_COLS)
        dc = np.clip(kc[None, :] - qc[:, None] + (WIN_COLS - 1), 0, 2 * WIN_COLS - 2)
        q_blk = qg[:, :, c0:c0 + Q_BLOCK_COLS]
        k_blk = k_rows[:, :, :, ks:ks + K_BLOCK_COLS]
        v_blk = v_rows[:, :, :, ks:ks + K_BLOCK_COLS]
        s = jnp.einsum('brqhd,brkchd->bhrqkc', q_blk, k_blk).astype(jnp.float32) * scale
        bias = rpb[:, dr[:, None, :, None], dc[None, :, None, :]].astype(jnp.float32)
        s = jnp.where(valid[:, None, :], s + bias[None], NEG_INF)
        p = jax.nn.softmax(s.reshape(s.shape[:4] + (kr * K_BLOCK_COLS,)), axis=-1)
        p = p.reshape(s.shape).astype(v.dtype)
        outs.append(jnp.einsum('bhrqkc,brkchd->brqhd', p, v_blk))
    return jnp.concatenate(outs, axis=2).reshape(B, T, H * d)


def mlstm_scan(q, k, v, i_pre, log_f):
    B, T, H, d = q.shape
    nc = T // CHUNK

    def to_chunks(a):
        return a.reshape(B, nc, CHUNK, H, d).transpose(1, 0, 3, 2, 4)

    def to_chunks_g(a):
        return a.reshape(B, nc, CHUNK, H).transpose(1, 0, 3, 2)

    tri = jnp.tril(jnp.ones((CHUNK, CHUNK), dtype=bool))

    def step(carry, inp):
        C, n, m = carry
        qc, kc, vc, ic, lfc = inp
        b = jnp.cumsum(lfc, axis=-1)
        D = jnp.where(tri, b[..., :, None] - b[..., None, :] + ic[..., None, :], NEG_INF)
        inter = b + m[..., None]
        m_t = jnp.maximum(inter, jnp.max(D, axis=-1))
        a = jnp.einsum('bhtd,bhsd->bhts', qc, kc) * jnp.exp(D - m_t[..., None])
        w_inter = jnp.exp(inter - m_t)
        num = w_inter[..., None] * jnp.einsum('bhtd,bhde->bhte', qc, C) + jnp.einsum('bhts,bhse->bhte', a, vc)
        den = w_inter * jnp.einsum('bhtd,bhd->bht', qc, n) + jnp.sum(a, axis=-1)
        h = num / jnp.maximum(jnp.abs(den), jnp.exp(-m_t))[..., None]
        bL = b[..., -1]
        g = bL[..., None] - b + ic
        m_new = jnp.maximum(bL + m, jnp.max(g, axis=-1))
        decay = jnp.exp(bL + m - m_new)
        kw = kc * jnp.exp(g - m_new[..., None])[..., None]
        C_new = decay[..., None, None] * C + jnp.einsum('bhsd,bhse->bhde', kw, vc)
        n_new = decay[..., None] * n + jnp.sum(kw, axis=2)
        return (C_new, n_new, m_new), h

    init = (jnp.zeros((B, H, d, d), jnp.float32),
            jnp.zeros((B, H, d), jnp.float32),
            jnp.full((B, H), NEG_INF, jnp.float32))
    _, h = lax.scan(step, init, (to_chunks(q), to_chunks(k), to_chunks(v), to_chunks_g(i_pre), to_chunks_g(log_f)))
    return h.transpose(1, 0, 3, 2, 4).reshape(B, T, H, d)


def mlstm_mixer(qk_m, v_m, o_m, gates, conv_w, gate_b, norm_g):
    B, T, _ = v_m.shape
    qk = jax.nn.silu(dwconv_centered(qk_m, conv_w)).astype(jnp.float32)
    q = qk[..., :D_MLSTM].reshape(B, T, N_HEADS_M, HD_M)
    k = qk[..., D_MLSTM:].reshape(B, T, N_HEADS_M, HD_M) * (HD_M ** -0.5)
    v = v_m.astype(jnp.float32).reshape(B, T, N_HEADS_M, HD_M)
    g = gates.astype(jnp.float32) + gate_b.astype(jnp.float32)
    i_fw, f_fw, i_bw, f_bw = jnp.split(g, 4, axis=-1)
    h_fw = mlstm_scan(q, k, v, i_fw, jax.nn.log_sigmoid(f_fw))

    def rev(a):
        return jnp.flip(a, axis=1)

    h_bw = rev(mlstm_scan(rev(q), rev(k), rev(v), rev(i_bw), rev(jax.nn.log_sigmoid(f_bw))))
    h = rms_norm(h_fw + h_bw, norm_g.reshape(N_HEADS_M, HD_M))
    return (jax.nn.sigmoid(o_m.astype(jnp.float32)) * h.reshape(B, T, D_MLSTM)).astype(v_m.dtype)


def encoder(x, norm_mix_g, w_in, mlstm_conv_w, gate_b, attn_rpb, attn_norm_g, mlstm_norm_g,
            w_out, norm_ffn_g, w_up, ffn_conv_w, w_down, norm_final_g):
    B, T, _ = x.shape
    cuts = [int(c) for c in np.cumsum(SPLIT_SIZES)[:-1]]
    for l in range(DEPTH):
        h = rms_norm(x, norm_mix_g[l])
        proj = h @ w_in[l]
        q_a, k_a, v_a, qk_m, v_m, o_m, gates = jnp.split(proj, cuts, axis=-1)
        y_att = neighbourhood_attention(q_a.reshape(B, T, N_HEADS_ATT, HD_ATT),
                                        k_a.reshape(B, T, N_HEADS_ATT, HD_ATT),
                                        v_a.reshape(B, T, N_HEADS_ATT, HD_ATT), attn_rpb[l])
        y_att = rms_norm(y_att, attn_norm_g[l])
        y_m = mlstm_mixer(qk_m, v_m, o_m, gates, mlstm_conv_w[l], gate_b[l], mlstm_norm_g[l])
        x = x + jnp.concatenate([y_att, y_m], axis=-1) @ w_out[l]
        h = rms_norm(x, norm_ffn_g[l])
        gate, val = jnp.split(h @ w_up[l], 2, axis=-1)
        x = x + (jax.nn.gelu(dwconv_centered(gate, ffn_conv_w[l])) * val) @ w_down[l]
    return rms_norm(x, norm_final_g)


def setup_inputs(seed: int = 0) -> dict:
    key = jax.random.key(seed)
    ks = jax.random.split(key, 20)
    f32 = jnp.float32

    def nrm(k, shape, s):
        return jax.random.normal(k, shape, f32) * s

    def gain(k, shape):
        return 1.0 + 0.01 * jax.random.normal(k, shape, f32)

    f_lin = jnp.linspace(3.0, 6.0, N_HEADS_M, dtype=f32)
    zeros_h = jnp.zeros((N_HEADS_M,), f32)
    gate_base = jnp.concatenate([zeros_h, f_lin, zeros_h, f_lin])
    return {
        'x_prompt': jax.random.normal(ks[0], (BATCH, SEQ, D_MODEL), f32),
        'x_sample': jax.random.normal(ks[1], (DEC_BATCH, DEC_SEQ, D_MODEL), f32),
        'norm_mix_g': gain(ks[2], (DEPTH, D_MODEL)),
        'w_in': nrm(ks[3], (DEPTH, D_MODEL, D_IN), D_MODEL ** -0.5),
        'mlstm_conv_w': nrm(ks[4], (DEPTH, CONV_W, 2 * D_MLSTM), CONV_W ** -0.5),
        'gate_b': gate_base[None, :] + nrm(ks[5], (DEPTH, N_GATES), 0.1),
        'attn_rpb': nrm(ks[6], (DEPTH, N_HEADS_ATT, 2 * WIN_ROWS - 1, 2 * WIN_COLS - 1), 0.1),
        'attn_norm_g': gain(ks[7], (DEPTH, D_ATT)),
        'mlstm_norm_g': gain(ks[8], (DEPTH, D_MLSTM)),
        'w_out': nrm(ks[9], (DEPTH, D_MIX, D_MODEL), D_MIX ** -0.5),
        'norm_ffn_g': gain(ks[10], (DEPTH, D_MODEL)),
        'w_up': nrm(ks[11], (DEPTH, D_MODEL, 2 * D_FF), D_MODEL ** -0.5),
        'ffn_conv_w': nrm(ks[12], (DEPTH, CONV_W, D_FF), CONV_W ** -0.5),
        'w_down': nrm(ks[13], (DEPTH, D_FF, D_MODEL), D_FF ** -0.5),
        'norm_final_g': gain(ks[14], (D_MODEL,)),
    }


def reference(x_prompt, x_sample, norm_mix_g, w_in, mlstm_conv_w, gate_b, attn_rpb, attn_norm_g,
              mlstm_norm_g, w_out, norm_ffn_g, w_up, ffn_conv_w, w_down, norm_final_g):
    y_prompt = encoder(x_prompt, norm_mix_g, w_in, mlstm_conv_w, gate_b, attn_rpb, attn_norm_g,
                       mlstm_norm_g, w_out, norm_ffn_g, w_up, ffn_conv_w, w_down, norm_final_g)
    y_sample = encoder(x_sample, norm_mix_g, w_in, mlstm_conv_w, gate_b, attn_rpb, attn_norm_g,
                       mlstm_norm_g, w_out, norm_ffn_g, w_up, ffn_conv_w, w_down, norm_final_g)
    return (y_prompt, y_sample)
```

```python
import functools

import jax
import jax.numpy as jnp
import numpy as np
from jax import lax
from jax.experimental import pallas as pl
from jax.experimental.pallas import tpu as pltpu

F32 = jnp.float32
BF16 = jnp.bfloat16

D_MODEL = 1024
D_ATT = 512
N_HEADS_ATT = 8
HD_ATT = 64
D_MLSTM = 512
N_HEADS_M = 4
HD_M = 128
N_GATES = 16
D_FF = 2816
GRID_W = 64
WIN_ROWS = 8
WIN_COLS = 16
EPS = 1e-6
NEG_INF = -1e30

LANES = 128
BF16_SUBLANES = 16
VMEM_LIMIT = 56 * 1024 * 1024

TM_IN = 512
TM_FFN = 512
FF_CHUNK = 256
HALO = BF16_SUBLANES
ATT_ROWS_PER_STEP = 8
MLSTM_CHUNK = 128
KEYS = WIN_ROWS * GRID_W


def _const_spec(shape):
    nd = len(shape)
    return pl.BlockSpec(shape, lambda *_: (0,) * nd, pipeline_mode=pl.Buffered(1))


def _rms(x, g):
    return x * lax.rsqrt(jnp.mean(x * x, axis=-1, keepdims=True) + EPS) * g


def _dot(a, b):
    return jnp.dot(a, b, preferred_element_type=F32)


def _dot_nt(a, b):
    return lax.dot_general(a, b, (((1,), (1,)), ((), ())), preferred_element_type=F32)


def _dot_tn(a, b):
    return lax.dot_general(a, b, (((0,), (0,)), ((), ())), preferred_element_type=F32)


def _inproj_kernel(x_ref, g_ref, wa_ref, wqk_ref, wv_ref, wo_ref, wg_ref,
                   qkv_ref, qkm_ref, vm_ref, om_ref, gates_ref):
    h = _rms(x_ref[...], g_ref[...]).astype(BF16)
    qkv_ref[...] = _dot(h, wa_ref[...]).astype(BF16)
    qkm_ref[...] = _dot(h, wqk_ref[...]).astype(BF16)
    vm_ref[...] = _dot(h, wv_ref[...]).astype(BF16)
    om_ref[...] = _dot(h, wo_ref[...]).astype(BF16)
    gates_ref[...] = _dot(h, wg_ref[...])[:, :N_GATES]


def _in_projection(x2, norm_g, w_in):
    n = x2.shape[0]
    assert n % TM_IN == 0
    c0, c1, c2, c3, c4 = 3 * D_ATT, 3 * D_ATT + 2 * D_MLSTM, 3 * D_ATT + 3 * D_MLSTM, 3 * D_ATT + 4 * D_MLSTM, 0
    wb = w_in.astype(BF16)
    wa, wqk, wv, wo = wb[:, :c0], wb[:, c0:c1], wb[:, c1:c2], wb[:, c2:c3]
    wg = jnp.pad(wb[:, c3:], ((0, 0), (0, LANES - N_GATES)))
    row = lambda w: pl.BlockSpec((TM_IN, w), lambda i: (i, 0))
    return pl.pallas_call(
        _inproj_kernel,
        grid=(n // TM_IN,),
        in_specs=[row(D_MODEL), _const_spec((1, D_MODEL)), _const_spec(wa.shape), _const_spec(wqk.shape),
                  _const_spec(wv.shape), _const_spec(wo.shape), _const_spec(wg.shape)],
        out_specs=[row(3 * D_ATT), row(2 * D_MLSTM), row(D_MLSTM), row(D_MLSTM), row(N_GATES)],
        out_shape=[jax.ShapeDtypeStruct((n, 3 * D_ATT), BF16), jax.ShapeDtypeStruct((n, 2 * D_MLSTM), BF16),
                   jax.ShapeDtypeStruct((n, D_MLSTM), BF16), jax.ShapeDtypeStruct((n, D_MLSTM), BF16),
                   jax.ShapeDtypeStruct((n, N_GATES), F32)],
        compiler_params=pltpu.CompilerParams(dimension_semantics=("arbitrary",), vmem_limit_bytes=VMEM_LIMIT),
        name="in_projection",
    )(x2, norm_g.reshape(1, D_MODEL), wa, wqk, wv, wo, wg)


def _gate_prep_kernel(g_ref, gb_ref, out_ref, *, chunk):
    t = g_ref.shape[2]
    gi = g_ref[0] + gb_ref[0]
    gf = g_ref[1] + gb_ref[1]
    pos = lax.broadcasted_iota(jnp.int32, (8, t), 1) & (chunk - 1)
    causal = lax.broadcasted_iota(jnp.int32, (8, t), 0) < N_HEADS_M
    lf = jnp.minimum(gf, 0.0) - jnp.log(1.0 + jnp.exp(-jnp.abs(gf)))

    pre, suf = lf, lf
    s = 1
    while s < chunk:
        pre = pre + jnp.where(pos >= s, pltpu.roll(pre, s, 1), 0.0)
        suf = suf + jnp.where(pos < chunk - s, pltpu.roll(suf, t - s, 1), 0.0)
        s *= 2
    b = jnp.where(causal, pre, suf)
    rt = gi - b

    def allreduce(z, op):
        s = 1
        while s < chunk:
            z = op(z, jnp.where((pos & s) == 0, pltpu.roll(z, t - s, 1), pltpu.roll(z, s, 1)))
            s *= 2
        return z

    tot = allreduce(lf, jnp.add)
    out_ref[0] = b
    out_ref[1] = rt
    out_ref[2] = tot + allreduce(rt, jnp.maximum)
    out_ref[3] = tot


def _gate_prep(gates, gate_b, chunk):
    b, t, _ = gates.shape
    assert chunk & (chunk - 1) == 0 and t % chunk == 0
    g = gates.reshape(b, t, 2, 2, N_HEADS_M).transpose(0, 3, 2, 4, 1).reshape(b, 2, 8, t)
    gb = gate_b.reshape(2, 2, N_HEADS_M).transpose(1, 0, 2).reshape(2, 8, 1)
    out = pl.pallas_call(
        functools.partial(_gate_prep_kernel, chunk=chunk),
        grid=(b,),
        in_specs=[pl.BlockSpec((None, 2, 8, t), lambda i: (i, 0, 0, 0)), _const_spec((2, 8, 1))],
        out_specs=pl.BlockSpec((None, 4, 8, t), lambda i: (i, 0, 0, 0)),
        out_shape=jax.ShapeDtypeStruct((b, 4, 8, t), F32),
        compiler_params=pltpu.CompilerParams(dimension_semantics=("arbitrary",), vmem_limit_bytes=VMEM_LIMIT),
        name="gate_prep",
    )(g, gb)
    out = out.reshape(b, 4, 2, N_HEADS_M, t // chunk, chunk)
    grow = out.transpose(0, 3, 4, 1, 2, 5).reshape(b, N_HEADS_M, t // chunk, 8, chunk)
    gcol = grow.swapaxes(3, 4).reshape(b, N_HEADS_M, t, 8)
    return grow, gcol


def _attention_kernel(q_ref, k_ref, v_ref, bias_ref, g_ref, o_ref, *, rows):
    step = pl.program_id(1)
    lane = lax.broadcasted_iota(jnp.int32, (GRID_W, LANES), 1)
    low_half = lane < HD_ATT
    scale = HD_ATT ** -0.5

    def one_row(rr, carry):
        r = step * ATT_ROWS_PER_STEP + rr
        rs = jnp.clip(r - WIN_ROWS // 2, 0, rows - WIN_ROWS)
        off = r - rs
        q_all = q_ref[pl.ds(pl.multiple_of(rr * GRID_W, GRID_W), GRID_W), :] * scale
        key0 = pl.multiple_of(rs * GRID_W, GRID_W)
        outs = []
        for p in range(N_HEADS_ATT // 2):
            cols = slice(p * LANES, (p + 1) * LANES)
            q2 = q_all[:, cols]
            k2 = k_ref[pl.ds(key0, KEYS), cols]
            v2 = v_ref[pl.ds(key0, KEYS), cols]
            halves = []
            for e in range(2):
                qm = jnp.where(low_half if e == 0 else jnp.logical_not(low_half), q2, jnp.zeros_like(q2))
                s = _dot_nt(qm, k2) + bias_ref[off, 2 * p + e]
                m = jnp.max(s, axis=-1, keepdims=True)
                pexp = jnp.exp(s - m)
                l = jnp.sum(pexp, axis=-1, keepdims=True)
                halves.append(_dot(pexp.astype(BF16), v2) / l)
            outs.append(jnp.where(low_half, halves[0], halves[1]))
        y = jnp.concatenate(outs, axis=1)
        o_ref[pl.ds(pl.multiple_of(rr * GRID_W, GRID_W), GRID_W), :] = _rms(y, g_ref[...]).astype(BF16)
        return carry

    lax.fori_loop(0, ATT_ROWS_PER_STEP, one_row, 0)


def _attention_bias_table(rpb):
    off = np.arange(WIN_ROWS)[:, None]
    kr = np.arange(WIN_ROWS)[None, :]
    dr = kr - off + (WIN_ROWS - 1)
    c = np.arange(GRID_W)
    cstart = np.clip(c - WIN_COLS // 2, 0, GRID_W - WIN_COLS)
    valid = (c[None, :] >= cstart[:, None]) & (c[None, :] < cstart[:, None] + WIN_COLS)
    dc = np.clip(c[None, :] - c[:, None] + (WIN_COLS - 1), 0, 2 * WIN_COLS - 2)
    tab = rpb.astype(F32)[:, dr[:, None, :, None], dc[None, :, None, :]]
    tab = jnp.where(valid[None, None, :, None, :], tab, NEG_INF)
    return tab.transpose(1, 0, 2, 3, 4).reshape(WIN_ROWS, N_HEADS_ATT, GRID_W, KEYS)


def _attention(qkv, rpb, norm_g):
    b, t, _ = qkv.shape
    rows = t // GRID_W
    assert rows >= WIN_ROWS and rows % ATT_ROWS_PER_STEP == 0
    tq = ATT_ROWS_PER_STEP * GRID_W
    bias = _attention_bias_table(rpb)
    return pl.pallas_call(
        functools.partial(_attention_kernel, rows=rows),
        grid=(b, rows // ATT_ROWS_PER_STEP),
        in_specs=[pl.BlockSpec((None, tq, D_ATT), lambda i, j: (i, j, 0)),
                  pl.BlockSpec((None, t, D_ATT), lambda i, j: (i, 0, 1)),
                  pl.BlockSpec((None, t, D_ATT), lambda i, j: (i, 0, 2)),
                  _const_spec(bias.shape), _const_spec((1, D_ATT))],
        out_specs=pl.BlockSpec((None, tq, D_ATT), lambda i, j: (i, j, 0)),
        out_shape=jax.ShapeDtypeStruct((b, t, D_ATT), BF16),
        compiler_params=pltpu.CompilerParams(dimension_semantics=("arbitrary", "arbitrary"),
                                             vmem_limit_bytes=VMEM_LIMIT),
        name="neighbourhood_attention",
    )(qkv, qkv, qkv, bias, norm_g.reshape(1, D_ATT))


def _mlstm_kernel(q_ref, k_ref, v_ref, o_ref, wq_ref, wk_ref, gcol_ref, grow_ref, ng_ref, y_ref,
                  qs_ref, ks_ref, hf_ref, hb_ref, c_ref, *, chunk):
    L = chunk
    t = q_ref.shape[0]
    nc = t // L
    row_i = lax.broadcasted_iota(jnp.int32, (L, LANES), 0)
    ti = lax.broadcasted_iota(jnp.int32, (L, L), 0)
    si = lax.broadcasted_iota(jnp.int32, (L, L), 1)
    lane1 = lax.broadcasted_iota(jnp.int32, (L, LANES), 1)
    ones_col = jnp.where(lane1 == 0, 1.0, 0.0).astype(BF16)

    def conv_silu(ref, w_ref, c, mul):
        start = pl.multiple_of(c * L, L)
        x = ref[pl.ds(start, L), :].astype(F32)
        pstart = pl.multiple_of(jnp.maximum(start - HALO, 0), HALO)
        nstart = pl.multiple_of(jnp.minimum(start + L, t - HALO), HALO)
        prev = ref[pl.ds(pstart, HALO), :].astype(F32)[HALO - 1:HALO, :]
        nxt = ref[pl.ds(nstart, HALO), :].astype(F32)[0:1, :]
        prev = jnp.where(c > 0, prev, 0.0)
        nxt = jnp.where(c < nc - 1, nxt, 0.0)
        xm = jnp.where(row_i == 0, prev, pltpu.roll(x, 1, 0))
        xp = jnp.where(row_i == L - 1, nxt, pltpu.roll(x, L - 1, 0))
        w = w_ref[...]
        y = xm * w[0:1, :] + x * w[1:2, :] + xp * w[2:3, :]
        return (y * jax.nn.sigmoid(y) * mul).astype(BF16)

    def phase0(c, carry):
        start = pl.multiple_of(c * L, L)
        qs_ref[pl.ds(start, L), :] = conv_silu(q_ref, wq_ref, c, 1.0)
        ks_ref[pl.ds(start, L), :] = conv_silu(k_ref, wk_ref, c, HD_M ** -0.5)
        return carry

    lax.fori_loop(0, nc, phase0, 0)
    c_ref[...] = jnp.zeros_like(c_ref)

    def chunk_step(c, d, m_prev):
        start = pl.multiple_of(c * L, L)
        q = qs_ref[pl.ds(start, L), :]
        k = ks_ref[pl.ds(start, L), :]
        v = v_ref[pl.ds(start, L), :]
        gc = gcol_ref[pl.ds(start, L), :]
        gr = grow_ref[c]
        bcol = gc[:, d:d + 1]
        rtcol = gc[:, 2 + d:3 + d]
        rtrow = gr[2 + d:3 + d, :]
        maxg = gc[0:1, 4 + d:5 + d]
        tot = gc[0:1, 6 + d:7 + d]
        mask = (si <= ti) if d == 0 else (si >= ti)
        dmat = jnp.where(mask, bcol + rtrow, NEG_INF)
        inter = bcol + m_prev
        m_t = jnp.maximum(inter, jnp.max(dmat, axis=-1, keepdims=True))
        a = _dot_nt(q, k) * jnp.exp(dmat - m_t)
        w_inter = jnp.exp(inter - m_t)
        v_ext = jnp.concatenate([v, ones_col], axis=1)
        c_old = c_ref[d]
        nd = w_inter * _dot(q, c_old.astype(BF16)) + _dot(a.astype(BF16), v_ext)
        den = nd[:, HD_M:HD_M + 1]
        h = nd[:, :HD_M] / jnp.maximum(jnp.abs(den), jnp.exp(-m_t))
        m_new = jnp.maximum(tot + m_prev, maxg)
        decay = jnp.exp(tot + m_prev - m_new)
        kw = (k.astype(F32) * jnp.exp(tot + rtcol - m_new)).astype(BF16)
        c_ref[d] = decay * c_old + _dot_tn(kw, v_ext)
        return h, m_new

    def phase1(j, ms):
        m_fw, m_bw = ms
        jb = nc - 1 - j
        h_fw, m_fw = chunk_step(j, 0, m_fw)
        hf_ref[pl.ds(pl.multiple_of(j * L, L), L), :] = h_fw
        h_bw, m_bw = chunk_step(jb, 1, m_bw)
        hb_ref[pl.ds(pl.multiple_of(jb * L, L), L), :] = h_bw
        return m_fw, m_bw

    m0 = jnp.full((1, 1), NEG_INF, F32)
    lax.fori_loop(0, nc, phase1, (m0, m0))

    def phase2(c, carry):
        start = pl.multiple_of(c * L, L)
        h = _rms(hf_ref[pl.ds(start, L), :] + hb_ref[pl.ds(start, L), :], ng_ref[...])
        og = jax.nn.sigmoid(o_ref[pl.ds(start, L), :].astype(F32))
        y_ref[pl.ds(start, L), :] = (og * h).astype(BF16)
        return carry

    lax.fori_loop(0, nc, phase2, 0)


def _mlstm(qkm, vm, om, grow, gcol, conv_w, norm_g, chunk):
    b, t, _ = vm.shape
    assert t % chunk == 0
    head = lambda off: pl.BlockSpec((None, t, HD_M), lambda i, h: (i, 0, h + off))
    return pl.pallas_call(
        functools.partial(_mlstm_kernel, chunk=chunk),
        grid=(b, N_HEADS_M),
        in_specs=[head(0), head(N_HEADS_M), head(0), head(0),
                  pl.BlockSpec((3, HD_M), lambda i, h: (0, h)),
                  pl.BlockSpec((3, HD_M), lambda i, h: (0, h + N_HEADS_M)),
                  pl.BlockSpec((None, None, t, 8), lambda i, h: (i, h, 0, 0)),
                  pl.BlockSpec((None, None, t // chunk, 8, chunk), lambda i, h: (i, h, 0, 0, 0)),
                  pl.BlockSpec((1, HD_M), lambda i, h: (0, h))],
        out_specs=head(0),
        out_shape=jax.ShapeDtypeStruct((b, t, D_MLSTM), BF16),
        scratch_shapes=[pltpu.VMEM((t, HD_M), BF16), pltpu.VMEM((t, HD_M), BF16),
                        pltpu.VMEM((t, HD_M), F32), pltpu.VMEM((t, HD_M), F32),
                        pltpu.VMEM((2, HD_M, 2 * HD_M), F32)],
        compiler_params=pltpu.CompilerParams(dimension_semantics=("arbitrary", "arbitrary"),
                                             vmem_limit_bytes=VMEM_LIMIT),
        name="mlstm",
    )(qkm, qkm, vm, om, conv_w, conv_w, gcol, grow, norm_g.reshape(1, D_MLSTM))


def _gelu_tanh(x):
    return 0.5 * x * (1.0 + jnp.tanh(np.sqrt(2.0 / np.pi).astype(np.float32) * (x + 0.044715 * (x * x * x))))


def _ffn_kernel(x_ref, xp_ref, xn_ref, ya_ref, yap_ref, yan_ref, ym_ref, ymp_ref, ymn_ref,
                woa_ref, wom_ref, gffn_ref, wup_ref, cw_ref, wdn_ref, gfin_ref, out_ref, *, tiles_per_seq):
    i = pl.program_id(0)
    first = (i % tiles_per_seq) == 0
    last = (i % tiles_per_seq) == tiles_per_seq - 1
    tm = x_ref.shape[0]

    def mixed(xr, yar, ymr):
        return xr[...] + _dot(yar[...], woa_ref[...]) + _dot(ymr[...], wom_ref[...])

    x1 = mixed(x_ref, ya_ref, ym_ref)
    x1_ext = jnp.concatenate([mixed(xp_ref, yap_ref, ymp_ref), x1, mixed(xn_ref, yan_ref, ymn_ref)], axis=0)
    h_ext = _rms(x1_ext, gffn_ref[...]).astype(BF16)
    h = h_ext[HALO:HALO + tm]
    te = tm + 2 * HALO
    row_e = lax.broadcasted_iota(jnp.int32, (te, FF_CHUNK), 0)
    keep = jnp.logical_and(jnp.logical_or(row_e >= HALO, jnp.logical_not(first)),
                           jnp.logical_or(row_e < HALO + tm, jnp.logical_not(last)))

    def ff_chunk(c, acc):
        gate = jnp.where(keep, _dot(h_ext, wup_ref[0, c]), 0.0)
        val = _dot(h, wup_ref[1, c])
        cw = cw_ref[c]
        conv = (pltpu.roll(gate, 1, 0) * cw[0:1, :] + gate * cw[1:2, :]
                + pltpu.roll(gate, te - 1, 0) * cw[2:3, :])[HALO:HALO + tm]
        act = (_gelu_tanh(conv) * val).astype(BF16)
        return acc + _dot(act, wdn_ref[c])

    x2 = lax.fori_loop(0, D_FF // FF_CHUNK, ff_chunk, x1)
    out_ref[...] = _rms(x2, gfin_ref[...])


def _out_ffn(x2d, ya, ym, seq_len, w_out, norm_ffn_g, w_up, conv_w, w_down, norm_final_g):
    n = x2d.shape[0]
    tm = TM_FFN
    assert seq_len % tm == 0 and tm % HALO == 0 and D_FF % FF_CHUNK == 0
    r = tm // HALO
    nh = n // HALO
    main = lambda w: pl.BlockSpec((tm, w), lambda i: (i, 0))
    prev = lambda w: pl.BlockSpec((HALO, w), lambda i: (jnp.maximum(i * r - 1, 0), 0))
    nxt = lambda w: pl.BlockSpec((HALO, w), lambda i: (jnp.minimum((i + 1) * r, nh - 1), 0))
    wo = w_out.astype(BF16)
    nch = D_FF // FF_CHUNK
    wup = w_up.astype(BF16).reshape(D_MODEL, 2, nch, FF_CHUNK).transpose(1, 2, 0, 3)
    cw = conv_w.reshape(3, nch, FF_CHUNK).swapaxes(0, 1)
    wdn = w_down.astype(BF16).reshape(nch, FF_CHUNK, D_MODEL)
    return pl.pallas_call(
        functools.partial(_ffn_kernel, tiles_per_seq=seq_len // tm),
        grid=(n // tm,),
        in_specs=[main(D_MODEL), prev(D_MODEL), nxt(D_MODEL),
                  main(D_ATT), prev(D_ATT), nxt(D_ATT),
                  main(D_MLSTM), prev(D_MLSTM), nxt(D_MLSTM),
                  _const_spec((D_ATT, D_MODEL)), _const_spec((D_MLSTM, D_MODEL)), _const_spec((1, D_MODEL)),
                  _const_spec((2, nch, D_MODEL, FF_CHUNK)), _const_spec((nch, 3, FF_CHUNK)),
                  _const_spec((nch, FF_CHUNK, D_MODEL)),
                  _const_spec((1, D_MODEL))],
        out_specs=main(D_MODEL),
        out_shape=jax.ShapeDtypeStruct((n, D_MODEL), F32),
        compiler_params=pltpu.CompilerParams(dimension_semantics=("arbitrary",), vmem_limit_bytes=VMEM_LIMIT),
        name="out_ffn",
    )(x2d, x2d, x2d, ya, ya, ya, ym, ym, ym, wo[:D_ATT], wo[D_ATT:], norm_ffn_g.reshape(1, D_MODEL),
      wup, cw, wdn, norm_final_g.reshape(1, D_MODEL))


def _encoder(x, norm_mix_g, w_in, mlstm_conv_w, gate_b, attn_rpb, attn_norm_g, mlstm_norm_g,
             w_out, norm_ffn_g, w_up, ffn_conv_w, w_down, norm_final_g):
    b, t, _ = x.shape
    n = b * t
    x2d = x.reshape(n, D_MODEL)
    qkv, qkm, vm, om, gates = _in_projection(x2d, norm_mix_g, w_in)
    grow, gcol = _gate_prep(gates.reshape(b, t, N_GATES), gate_b, MLSTM_CHUNK)
    y_att = _attention(qkv.reshape(b, t, 3 * D_ATT), attn_rpb, attn_norm_g)
    y_m = _mlstm(qkm.reshape(b, t, 2 * D_MLSTM), vm.reshape(b, t, D_MLSTM), om.reshape(b, t, D_MLSTM),
                 grow, gcol, mlstm_conv_w, mlstm_norm_g, MLSTM_CHUNK)
    out = _out_ffn(x2d, y_att.reshape(n, D_ATT), y_m.reshape(n, D_MLSTM), t,
                   w_out, norm_ffn_g, w_up, ffn_conv_w, w_down, norm_final_g)
    return out.reshape(b, t, D_MODEL)


def kernel(x_prompt, x_sample, norm_mix_g, w_in, mlstm_conv_w, gate_b, attn_rpb, attn_norm_g, mlstm_norm_g,
           w_out, norm_ffn_g, w_up, ffn_conv_w, w_down, norm_final_g):
    assert norm_mix_g.shape[0] == 1, "single-layer encoder"
    params = (norm_mix_g[0], w_in[0], mlstm_conv_w[0], gate_b[0], attn_rpb[0], attn_norm_g[0], mlstm_norm_g[0],
              w_out[0], norm_ffn_g[0], w_up[0], ffn_conv_w[0], w_down[0], norm_final_g)
    return (_encoder(x_prompt, *params), _encoder(x_sample, *params))
```

```python
import functools

import jax
import jax.numpy as jnp
import numpy as np
from jax import lax
from jax.experimental import pallas as pl
from jax.experimental.pallas import tpu as pltpu

F32 = jnp.float32
BF16 = jnp.bfloat16

D_MODEL = 1024
D_ATT = 512
N_HEADS_ATT = 8
HD_ATT = 64
D_MLSTM = 512
N_HEADS_M = 4
HD_M = 128
N_GATES = 16
D_FF = 2816
GRID_W = 64
WIN_ROWS = 8
WIN_COLS = 16
EPS = 1e-6
NEG_INF = -1e30

LANES = 128
BF16_SUBLANES = 16
VMEM_LIMIT = 56 * 1024 * 1024

TM_IN = 512
TM_FFN = 512
FF_CHUNK = 1408
HALO = BF16_SUBLANES
ATT_ROWS_PER_STEP = 8
MLSTM_CHUNK = 256
KEYS = WIN_ROWS * GRID_W


def _const_spec(shape):
    nd = len(shape)
    return pl.BlockSpec(shape, lambda *_: (0,) * nd, pipeline_mode=pl.Buffered(1))


def _rms(x, g):
    return x * lax.rsqrt(jnp.mean(x * x, axis=-1, keepdims=True) + EPS) * g


def _dot(a, b):
    return jnp.dot(a, b, preferred_element_type=F32)


def _dot_nt(a, b):
    return lax.dot_general(a, b, (((1,), (1,)), ((), ())), preferred_element_type=F32)


def _dot_tn(a, b):
    return lax.dot_general(a, b, (((0,), (0,)), ((), ())), preferred_element_type=F32)


def _inproj_kernel(x_ref, g_ref, wa_ref, wqk_ref, wv_ref, wo_ref, wg_ref,
                   qkv_ref, qkm_ref, vm_ref, om_ref, gates_ref):
    h = _rms(x_ref[...], g_ref[...]).astype(BF16)
    qkv_ref[...] = _dot(h, wa_ref[...]).astype(BF16)
    qkm_ref[...] = _dot(h, wqk_ref[...]).astype(BF16)
    vm_ref[...] = _dot(h, wv_ref[...]).astype(BF16)
    om_ref[...] = _dot(h, wo_ref[...]).astype(BF16)
    gates_ref[...] = _dot(h, wg_ref[...])[:, :N_GATES]


def _in_projection(x2, norm_g, w_in):
    n = x2.shape[0]
    assert n % TM_IN == 0
    c0, c1, c2, c3, c4 = 3 * D_ATT, 3 * D_ATT + 2 * D_MLSTM, 3 * D_ATT + 3 * D_MLSTM, 3 * D_ATT + 4 * D_MLSTM, 0
    wb = w_in.astype(BF16)
    wa, wqk, wv, wo = wb[:, :c0], wb[:, c0:c1], wb[:, c1:c2], wb[:, c2:c3]
    wg = jnp.pad(wb[:, c3:], ((0, 0), (0, LANES - N_GATES)))
    row = lambda w: pl.BlockSpec((TM_IN, w), lambda i: (i, 0))
    return pl.pallas_call(
        _inproj_kernel,
        grid=(n // TM_IN,),
        in_specs=[row(D_MODEL), _const_spec((1, D_MODEL)), _const_spec(wa.shape), _const_spec(wqk.shape),
                  _const_spec(wv.shape), _const_spec(wo.shape), _const_spec(wg.shape)],
        out_specs=[row(3 * D_ATT), row(2 * D_MLSTM), row(D_MLSTM), row(D_MLSTM), row(N_GATES)],
        out_shape=[jax.ShapeDtypeStruct((n, 3 * D_ATT), BF16), jax.ShapeDtypeStruct((n, 2 * D_MLSTM), BF16),
                   jax.ShapeDtypeStruct((n, D_MLSTM), BF16), jax.ShapeDtypeStruct((n, D_MLSTM), BF16),
                   jax.ShapeDtypeStruct((n, N_GATES), F32)],
        compiler_params=pltpu.CompilerParams(dimension_semantics=("arbitrary",), vmem_limit_bytes=VMEM_LIMIT),
        name="in_projection",
    )(x2, norm_g.reshape(1, D_MODEL), wa, wqk, wv, wo, wg)


def _gate_prep_kernel(g_ref, gb_ref, out_ref, *, chunk):
    t = g_ref.shape[2]
    gi = g_ref[0] + gb_ref[0]
    gf = g_ref[1] + gb_ref[1]
    pos = lax.broadcasted_iota(jnp.int32, (8, t), 1) & (chunk - 1)
    causal = lax.broadcasted_iota(jnp.int32, (8, t), 0) < N_HEADS_M
    lf = jnp.minimum(gf, 0.0) - jnp.log(1.0 + jnp.exp(-jnp.abs(gf)))

    pre, suf = lf, lf
    s = 1
    while s < chunk:
        pre = pre + jnp.where(pos >= s, pltpu.roll(pre, s, 1), 0.0)
        suf = suf + jnp.where(pos < chunk - s, pltpu.roll(suf, t - s, 1), 0.0)
        s *= 2
    b = jnp.where(causal, pre, suf)
    rt = gi - b

    def allreduce(z, op):
        s = 1
        while s < chunk:
            z = op(z, jnp.where((pos & s) == 0, pltpu.roll(z, t - s, 1), pltpu.roll(z, s, 1)))
            s *= 2
        return z

    tot = allreduce(lf, jnp.add)
    out_ref[0] = b
    out_ref[1] = rt
    out_ref[2] = tot + allreduce(rt, jnp.maximum)
    out_ref[3] = tot


def _gate_prep(gates, gate_b, chunk):
    b, t, _ = gates.shape
    assert chunk & (chunk - 1) == 0 and t % chunk == 0
    g = gates.reshape(b, t, 2, 2, N_HEADS_M).transpose(0, 3, 2, 4, 1).reshape(b, 2, 8, t)
    gb = gate_b.reshape(2, 2, N_HEADS_M).transpose(1, 0, 2).reshape(2, 8, 1)
    out = pl.pallas_call(
        functools.partial(_gate_prep_kernel, chunk=chunk),
        grid=(b,),
        in_specs=[pl.BlockSpec((None, 2, 8, t), lambda i: (i, 0, 0, 0)), _const_spec((2, 8, 1))],
        out_specs=pl.BlockSpec((None, 4, 8, t), lambda i: (i, 0, 0, 0)),
        out_shape=jax.ShapeDtypeStruct((b, 4, 8, t), F32),
        compiler_params=pltpu.CompilerParams(dimension_semantics=("arbitrary",), vmem_limit_bytes=VMEM_LIMIT),
        name="gate_prep",
    )(g, gb)
    out = out.reshape(b, 4, 2, N_HEADS_M, t // chunk, chunk)
    grow = out.transpose(0, 3, 4, 1, 2, 5).reshape(b, N_HEADS_M, t // chunk, 8, chunk)
    gcol = grow.swapaxes(3, 4).reshape(b, N_HEADS_M, t, 8)
    return grow, gcol


def _attention_kernel(q_ref, k_ref, v_ref, bias_ref, g_ref, o_ref, *, rows):
    step = pl.program_id(1)
    lane = lax.broadcasted_iota(jnp.int32, (GRID_W, LANES), 1)
    low_half = lane < HD_ATT
    scale = HD_ATT ** -0.5
    ones_col = jnp.where(lax.broadcasted_iota(jnp.int32, (KEYS, LANES), 1) == 0, 1.0, 0.0).astype(BF16)

    def one_row(rr, carry):
        r = step * ATT_ROWS_PER_STEP + rr
        rs = jnp.clip(r - WIN_ROWS // 2, 0, rows - WIN_ROWS)
        off = r - rs
        q_all = q_ref[pl.ds(pl.multiple_of(rr * GRID_W, GRID_W), GRID_W), :] * scale
        key0 = pl.multiple_of(rs * GRID_W, GRID_W)
        pairs = range(N_HEADS_ATT // 2)
        cols = [slice(p * LANES, (p + 1) * LANES) for p in pairs]
        zero = jnp.zeros((GRID_W, LANES), BF16)
        qs = [jnp.concatenate([jnp.where(low_half, q_all[:, c], zero), jnp.where(low_half, zero, q_all[:, c])],
                              axis=0) for c in cols]
        s = [_dot_nt(qs[p], k_ref[pl.ds(key0, KEYS), cols[p]]) + bias_ref[off, p] for p in pairs]
        m = [jnp.max(s[p], axis=-1, keepdims=True) for p in pairs]
        pexp = [jnp.exp(s[p] - m[p]).astype(BF16) for p in pairs]
        o2 = [_dot(pexp[p], jnp.concatenate([v_ref[pl.ds(key0, KEYS), cols[p]], ones_col], axis=1)) for p in pairs]
        o2 = [o2[p][:, :LANES] / o2[p][:, LANES:LANES + 1] for p in pairs]
        y = jnp.concatenate([jnp.where(low_half, o[:GRID_W], o[GRID_W:]) for o in o2], axis=1)
        o_ref[pl.ds(pl.multiple_of(rr * GRID_W, GRID_W), GRID_W), :] = _rms(y, g_ref[...]).astype(BF16)
        return carry

    lax.fori_loop(0, ATT_ROWS_PER_STEP, one_row, 0, unroll=True)


def _attention_bias_table(rpb):
    c = np.arange(GRID_W)
    cstart = np.clip(c - WIN_COLS // 2, 0, GRID_W - WIN_COLS)
    valid = (c[None, :] >= cstart[:, None]) & (c[None, :] < cstart[:, None] + WIN_COLS)
    dc = np.clip(c[None, :] - c[:, None] + (WIN_COLS - 1), 0, 2 * WIN_COLS - 2)
    onehot = ((dc[None] == np.arange(2 * WIN_COLS - 1)[:, None, None]) & valid[None]).astype(np.float32)
    t15 = jnp.einsum('hdj,jck->hdck', rpb.astype(F32), onehot, precision=lax.Precision.HIGHEST)
    t15 = jnp.where(valid, t15, NEG_INF)
    tab = jnp.stack([t15[:, WIN_ROWS - 1 - o:2 * WIN_ROWS - 1 - o] for o in range(WIN_ROWS)])
    return tab.transpose(0, 1, 3, 2, 4).reshape(WIN_ROWS, N_HEADS_ATT // 2, 2 * GRID_W, KEYS)


def _attention(qkv, rpb, norm_g):
    b, t, _ = qkv.shape
    rows = t // GRID_W
    assert rows >= WIN_ROWS and rows % ATT_ROWS_PER_STEP == 0
    tq = ATT_ROWS_PER_STEP * GRID_W
    bias = _attention_bias_table(rpb)
    return pl.pallas_call(
        functools.partial(_attention_kernel, rows=rows),
        grid=(b, rows // ATT_ROWS_PER_STEP),
        in_specs=[pl.BlockSpec((None, tq, D_ATT), lambda i, j: (i, j, 0)),
                  pl.BlockSpec((None, t, D_ATT), lambda i, j: (i, 0, 1)),
                  pl.BlockSpec((None, t, D_ATT), lambda i, j: (i, 0, 2)),
                  _const_spec(bias.shape), _const_spec((1, D_ATT))],
        out_specs=pl.BlockSpec((None, tq, D_ATT), lambda i, j: (i, j, 0)),
        out_shape=jax.ShapeDtypeStruct((b, t, D_ATT), BF16),
        compiler_params=pltpu.CompilerParams(dimension_semantics=("arbitrary", "arbitrary"),
                                             vmem_limit_bytes=VMEM_LIMIT),
        name="neighbourhood_attention",
    )(qkv, qkv, qkv, bias, norm_g.reshape(1, D_ATT))


def _mlstm_kernel(q_ref, k_ref, v_ref, o_ref, wq_ref, wk_ref, gcol_ref, grow_ref, sel_ref, ng_ref, y_ref,
                  qs_ref, ks_ref, hf_ref, hb_ref, c_ref, *, chunk):
    L = chunk
    t = q_ref.shape[0]
    nc = t // L
    row_i = lax.broadcasted_iota(jnp.int32, (L, LANES), 0)
    ti = lax.broadcasted_iota(jnp.int32, (L, L), 0)
    si = lax.broadcasted_iota(jnp.int32, (L, L), 1)
    ones_blk = jnp.ones((L, HD_M), BF16)

    def wide(x, n):
        return jnp.concatenate([x] * (n // LANES), axis=1)

    def conv_silu(ref, w_ref, c, mul):
        start = pl.multiple_of(c * L, L)
        x = ref[pl.ds(start, L), :].astype(F32)
        pstart = pl.multiple_of(jnp.maximum(start - HALO, 0), HALO)
        nstart = pl.multiple_of(jnp.minimum(start + L, t - HALO), HALO)
        prev = ref[pl.ds(pstart, HALO), :].astype(F32)[HALO - 1:HALO, :]
        nxt = ref[pl.ds(nstart, HALO), :].astype(F32)[0:1, :]
        prev = jnp.where(c > 0, prev, 0.0)
        nxt = jnp.where(c < nc - 1, nxt, 0.0)
        xm = jnp.where(row_i == 0, prev, pltpu.roll(x, 1, 0))
        xp = jnp.where(row_i == L - 1, nxt, pltpu.roll(x, L - 1, 0))
        w = w_ref[...]
        y = xm * w[0:1, :] + x * w[1:2, :] + xp * w[2:3, :]
        return (y * jax.nn.sigmoid(y) * mul).astype(BF16)

    def phase0(c, carry):
        start = pl.multiple_of(c * L, L)
        qs_ref[pl.ds(start, L), :] = conv_silu(q_ref, wq_ref, c, 1.0)
        ks_ref[pl.ds(start, L), :] = conv_silu(k_ref, wk_ref, c, HD_M ** -0.5)
        return carry

    lax.fori_loop(0, nc, phase0, 0)
    c_ref[...] = jnp.zeros_like(c_ref)

    def chunk_step(c, d, m_prev):
        start = pl.multiple_of(c * L, L)
        q = qs_ref[pl.ds(start, L), :]
        k = ks_ref[pl.ds(start, L), :]
        v = v_ref[pl.ds(start, L), :]
        gr = grow_ref[c]
        cb = _dot(gcol_ref[pl.ds(start, L), :], sel_ref[d])
        bcol, rtcol = cb[:, :LANES], cb[:, LANES:]
        rtrow = gr[2 + d:3 + d, :]
        maxg = gr[4 + d:5 + d, :LANES]
        tot = gr[6 + d:7 + d, :LANES]
        mask = (si <= ti) if d == 0 else (si >= ti)
        dmat = jnp.where(mask, wide(bcol, L) + rtrow, NEG_INF)
        inter = bcol + m_prev
        m_t = jnp.maximum(inter, jnp.max(dmat, axis=-1, keepdims=True))
        a = _dot_nt(q, k) * jnp.exp(dmat - wide(m_t, L))
        w_inter = jnp.exp(inter - m_t)
        v_ext = jnp.concatenate([v, ones_blk], axis=1)
        c_old = c_ref[d]
        nd = wide(w_inter, 2 * HD_M) * _dot(q, c_old.astype(BF16)) + _dot(a.astype(BF16), v_ext)
        h = nd[:, :HD_M] / jnp.maximum(jnp.abs(nd[:, HD_M:]), jnp.exp(-m_t))
        m_new = jnp.maximum(tot + m_prev, maxg)
        decay = jnp.exp(tot + m_prev - m_new)
        kw = (k.astype(F32) * jnp.exp(tot + rtcol - m_new)).astype(BF16)
        c_ref[d] = wide(decay, 2 * HD_M) * c_old + _dot_tn(kw, v_ext)
        return h, m_new

    def phase1(j, ms):
        m_fw, m_bw = ms
        jb = nc - 1 - j
        h_fw, m_fw = chunk_step(j, 0, m_fw)
        hf_ref[pl.ds(pl.multiple_of(j * L, L), L), :] = h_fw
        h_bw, m_bw = chunk_step(jb, 1, m_bw)
        hb_ref[pl.ds(pl.multiple_of(jb * L, L), L), :] = h_bw
        return m_fw, m_bw

    m0 = jnp.full((1, LANES), NEG_INF, F32)
    lax.fori_loop(0, nc, phase1, (m0, m0), unroll=2)

    def phase2(c, carry):
        start = pl.multiple_of(c * L, L)
        h = _rms(hf_ref[pl.ds(start, L), :] + hb_ref[pl.ds(start, L), :], ng_ref[...])
        og = jax.nn.sigmoid(o_ref[pl.ds(start, L), :].astype(F32))
        y_ref[pl.ds(start, L), :] = (og * h).astype(BF16)
        return carry

    lax.fori_loop(0, nc, phase2, 0)


def _split3(x):
    hi = x.astype(BF16)
    r = x - hi.astype(F32)
    mid = r.astype(BF16)
    lo = (r - mid.astype(F32)).astype(BF16)
    return jnp.concatenate([hi, mid, lo], axis=-1)


def _mlstm(qkm, vm, om, grow, gcol, conv_w, norm_g, chunk):
    b, t, _ = vm.shape
    assert t % chunk == 0
    gcol = jnp.pad(_split3(gcol), ((0, 0), (0, 0), (0, 0), (0, LANES - 24)))
    sel = np.zeros((2, LANES, 2 * LANES), np.float32)
    for d in range(2):
        for piece in range(3):
            sel[d, piece * 8 + d, :LANES] = 1.0
            sel[d, piece * 8 + 2 + d, LANES:] = 1.0
    sel = jnp.asarray(sel, BF16)
    head = lambda off: pl.BlockSpec((None, t, HD_M), lambda i, h: (i, 0, h + off))
    return pl.pallas_call(
        functools.partial(_mlstm_kernel, chunk=chunk),
        grid=(b, N_HEADS_M),
        in_specs=[head(0), head(N_HEADS_M), head(0), head(0),
                  pl.BlockSpec((3, HD_M), lambda i, h: (0, h)),
                  pl.BlockSpec((3, HD_M), lambda i, h: (0, h + N_HEADS_M)),
                  pl.BlockSpec((None, None, t, LANES), lambda i, h: (i, h, 0, 0)),
                  pl.BlockSpec((None, None, t // chunk, 8, chunk), lambda i, h: (i, h, 0, 0, 0)),
                  _const_spec(sel.shape),
                  pl.BlockSpec((1, HD_M), lambda i, h: (0, h))],
        out_specs=head(0),
        out_shape=jax.ShapeDtypeStruct((b, t, D_MLSTM), BF16),
        scratch_shapes=[pltpu.VMEM((t, HD_M), BF16), pltpu.VMEM((t, HD_M), BF16),
                        pltpu.VMEM((t, HD_M), F32), pltpu.VMEM((t, HD_M), F32),
                        pltpu.VMEM((2, HD_M, 2 * HD_M), F32)],
        compiler_params=pltpu.CompilerParams(dimension_semantics=("arbitrary", "arbitrary"),
                                             vmem_limit_bytes=VMEM_LIMIT),
        name="mlstm",
    )(qkm, qkm, vm, om, conv_w, conv_w, gcol, grow, sel, norm_g.reshape(1, D_MLSTM))


def _gelu_tanh(x):
    return 0.5 * x * (1.0 + jnp.tanh(np.sqrt(2.0 / np.pi).astype(np.float32) * (x + 0.044715 * (x * x * x))))


def _ffn_kernel(x_ref, xp_ref, xn_ref, ya_ref, yap_ref, yan_ref, ym_ref, ymp_ref, ymn_ref,
                woa_ref, wom_ref, gffn_ref, wup_ref, cw_ref, wdn_ref, gfin_ref, out_ref, *, tiles_per_seq):
    i = pl.program_id(0)
    first = (i % tiles_per_seq) == 0
    last = (i % tiles_per_seq) == tiles_per_seq - 1
    tm = x_ref.shape[0]

    ext = lambda p, m, n: jnp.concatenate([p[...], m[...], n[...]], axis=0)
    x1_ext = (ext(xp_ref, x_ref, xn_ref) + _dot(ext(yap_ref, ya_ref, yan_ref), woa_ref[...])
              + _dot(ext(ymp_ref, ym_ref, ymn_ref), wom_ref[...]))
    x1 = x1_ext[HALO:HALO + tm]
    h_ext = _rms(x1_ext, gffn_ref[...]).astype(BF16)
    h = h_ext[HALO:HALO + tm]
    te = tm + 2 * HALO
    row_e = lax.broadcasted_iota(jnp.int32, (te, FF_CHUNK), 0)
    keep = jnp.logical_and(jnp.logical_or(row_e >= HALO, jnp.logical_not(first)),
                           jnp.logical_or(row_e < HALO + tm, jnp.logical_not(last)))

    def hidden(c):
        gate = jnp.where(keep, _dot(h_ext, wup_ref[0, c]), 0.0)
        val = _dot(h, wup_ref[1, c])
        cw = cw_ref[c]
        conv = (pltpu.roll(gate, 1, 0) * cw[0:1, :] + gate * cw[1:2, :]
                + pltpu.roll(gate, te - 1, 0) * cw[2:3, :])[HALO:HALO + tm]
        return (_gelu_tanh(conv) * val).astype(BF16)

    out_ref[...] = x1
    for c in range(D_FF // FF_CHUNK):
        out_ref[...] += _dot(hidden(c), wdn_ref[c])
    out_ref[...] = _rms(out_ref[...], gfin_ref[...])


def _out_ffn(x2d, ya, ym, seq_len, w_out, norm_ffn_g, w_up, conv_w, w_down, norm_final_g):
    n = x2d.shape[0]
    tm = TM_FFN
    assert seq_len % tm == 0 and tm % HALO == 0 and D_FF % FF_CHUNK == 0
    r = tm // HALO
    nh = n // HALO
    main = lambda w: pl.BlockSpec((tm, w), lambda i: (i, 0))
    prev = lambda w: pl.BlockSpec((HALO, w), lambda i: (jnp.maximum(i * r - 1, 0), 0))
    nxt = lambda w: pl.BlockSpec((HALO, w), lambda i: (jnp.minimum((i + 1) * r, nh - 1), 0))
    wo = w_out.astype(BF16)
    nch = D_FF // FF_CHUNK
    wup = w_up.astype(BF16).reshape(D_MODEL, 2, nch, FF_CHUNK).transpose(1, 2, 0, 3)
    cw = conv_w.reshape(3, nch, FF_CHUNK).swapaxes(0, 1)
    wdn = w_down.astype(BF16).reshape(nch, FF_CHUNK, D_MODEL)
    return pl.pallas_call(
        functools.partial(_ffn_kernel, tiles_per_seq=seq_len // tm),
        grid=(n // tm,),
        in_specs=[main(D_MODEL), prev(D_MODEL), nxt(D_MODEL),
                  main(D_ATT), prev(D_ATT), nxt(D_ATT),
                  main(D_MLSTM), prev(D_MLSTM), nxt(D_MLSTM),
                  _const_spec((D_ATT, D_MODEL)), _const_spec((D_MLSTM, D_MODEL)), _const_spec((1, D_MODEL)),
                  _const_spec((2, nch, D_MODEL, FF_CHUNK)), _const_spec((nch, 3, FF_CHUNK)),
                  _const_spec((nch, FF_CHUNK, D_MODEL)),
                  _const_spec((1, D_MODEL))],
        out_specs=main(D_MODEL),
        out_shape=jax.ShapeDtypeStruct((n, D_MODEL), F32),
        compiler_params=pltpu.CompilerParams(dimension_semantics=("arbitrary",), vmem_limit_bytes=VMEM_LIMIT),
        name="out_ffn",
    )(x2d, x2d, x2d, ya, ya, ya, ym, ym, ym, wo[:D_ATT], wo[D_ATT:], norm_ffn_g.reshape(1, D_MODEL),
      wup, cw, wdn, norm_final_g.reshape(1, D_MODEL))


def _encoder(x, norm_mix_g, w_in, mlstm_conv_w, gate_b, attn_rpb, attn_norm_g, mlstm_norm_g,
             w_out, norm_ffn_g, w_up, ffn_conv_w, w_down, norm_final_g):
    b, t, _ = x.shape
    n = b * t
    x2d = x.reshape(n, D_MODEL)
    qkv, qkm, vm, om, gates = _in_projection(x2d, norm_mix_g, w_in)
    grow, gcol = _gate_prep(gates.reshape(b, t, N_GATES), gate_b, MLSTM_CHUNK)
    y_att = _attention(qkv.reshape(b, t, 3 * D_ATT), attn_rpb, attn_norm_g)
    y_m = _mlstm(qkm.reshape(b, t, 2 * D_MLSTM), vm.reshape(b, t, D_MLSTM), om.reshape(b, t, D_MLSTM),
                 grow, gcol, mlstm_conv_w, mlstm_norm_g, MLSTM_CHUNK)
    out = _out_ffn(x2d, y_att.reshape(n, D_ATT), y_m.reshape(n, D_MLSTM), t,
                   w_out, norm_ffn_g, w_up, ffn_conv_w, w_down, norm_final_g)
    return out.reshape(b, t, D_MODEL)


def kernel(x_prompt, x_sample, norm_mix_g, w_in, mlstm_conv_w, gate_b, attn_rpb, attn_norm_g, mlstm_norm_g,
           w_out, norm_ffn_g, w_up, ffn_conv_w, w_down, norm_final_g):
    assert norm_mix_g.shape[0] == 1, "single-layer encoder"
    params = (norm_mix_g[0], w_in[0], mlstm_conv_w[0], gate_b[0], attn_rpb[0], attn_norm_g[0], mlstm_norm_g[0],
              w_out[0], norm_ffn_g[0], w_up[0], ffn_conv_w[0], w_down[0], norm_final_g)
    return (_encoder(x_prompt, *params), _encoder(x_sample, *params))
```
